```python
import jax, jax.numpy as jnp
from jax import lax
import numpy as np

D_MODEL = 1024
BATCH = 32
SEQ = 2048
DEPTH = 2

HEAD_DIM = D_MODEL // 16
ATTN_HEADS = 6
ATTN_WIDTH = ATTN_HEADS * HEAD_DIM
ATTN_PATTERNS = ((128, 1), (512, 4), (2048, 16))
HEADS_PER_PATTERN = ATTN_HEADS // len(ATTN_PATTERNS)
CONV_GROUPS = 4
CONV_WIDTH = CONV_GROUPS * HEAD_DIM
CONV_TAPS = 31
LRU_HEADS = 6
LRU_WIDTH = LRU_HEADS * HEAD_DIM
LRU_CONV_TAPS = 4
LRU_C = 8.0
MIX_WIDTH = ATTN_WIDTH + CONV_WIDTH + LRU_WIDTH
D_FF = 4 * D_MODEL
RMS_EPS = 1e-6
LN_EPS = 1e-5

Q0 = 0
K0 = Q0 + ATTN_WIDTH
V0 = K0 + ATTN_WIDTH
CA0 = V0 + ATTN_WIDTH
CG0 = CA0 + CONV_WIDTH
LG0 = CG0 + CONV_WIDTH
LX0 = LG0 + LRU_WIDTH
IN_COLS = LX0 + LRU_WIDTH

kernel_name = "hymba_style_conv_dilattn_rglru_block"


def rmsnorm(x, g):
    xf = x.astype(jnp.float32)
    y = xf * lax.rsqrt(jnp.mean(xf * xf, axis=-1, keepdims=True) + RMS_EPS)
    return y.astype(x.dtype) * g


def layernorm(x, g, b):
    xf = x.astype(jnp.float32)
    mu = jnp.mean(xf, axis=-1, keepdims=True)
    var = jnp.mean(jnp.square(xf - mu), axis=-1, keepdims=True)
    y = (xf - mu) * lax.rsqrt(var + LN_EPS)
    return y.astype(x.dtype) * g + b


def alibi_slopes(n):
    return jnp.asarray(2.0 ** (-8.0 * np.arange(1, n + 1) / n), dtype=jnp.float32)


def causal_depthwise_conv(x, w, b):
    K, C = w.shape
    y = lax.conv_general_dilated(x, w[:, None, :], window_strides=(1,), padding=[(K - 1, 0)],
                                 dimension_numbers=("NWC", "WIO", "NWC"), feature_group_count=C)
    return y + b


def dilated_window_attention(q, k, v, slopes, window, dilation):
    B, S, H, Dh = q.shape
    W = window // dilation
    L = S // dilation
    nb = -(-L // W)
    Lp = nb * W

    def to_blocks(t):
        t = t.reshape(B, L, dilation, H, Dh).transpose(0, 2, 3, 1, 4)
        t = jnp.pad(t, ((0, 0), (0, 0), (0, 0), (0, Lp - L), (0, 0)))
        return t.reshape(B, dilation, H, nb, W, Dh)

    def with_prev(t):
        prev = jnp.pad(t, ((0, 0), (0, 0), (0, 0), (1, 0), (0, 0), (0, 0)))[:, :, :, :-1]
        return jnp.concatenate([prev, t], axis=4)

    qb = to_blocks(q)
    kb = with_prev(to_blocks(k))
    vb = with_prev(to_blocks(v))
    s = jnp.einsum("brhnqc,brhnkc->brhnqk", qb, kb).astype(jnp.float32) * (Dh ** -0.5)
    qi = jnp.arange(W)[:, None]
    kj = jnp.arange(2 * W)[None, :]
    dist = qi + W - kj
    band = (dist >= 0) & (dist <= W)
    has_prev = (jnp.arange(nb)[:, None, None] > 0) | (kj[None] >= W)
    valid = band[None] & has_prev
    bias = -slopes[:, None, None] * (dilation * dist).astype(jnp.float32)[None]
    s = jnp.where(valid[None, None, None], s + bias[None, None, :, None], -jnp.inf)
    m = jnp.max(s, axis=-1, keepdims=True)
    p = jnp.exp(s - m)
    l = jnp.sum(p, axis=-1, keepdims=True)
    o = jnp.einsum("brhnqk,brhnkc->brhnqc", p, vb.astype(jnp.float32)) / l
    lse = (m + jnp.log(l))[..., 0]
    o = o.reshape(B, dilation, H, Lp, Dh)[:, :, :, :L].transpose(0, 3, 1, 2, 4).reshape(B, S, H, Dh)
    lse = lse.reshape(B, dilation, H, Lp)[..., :L].transpose(0, 3, 1, 2).reshape(B, S, H)
    return o, lse


def attention_mixer(q, k, v):
    B, S = q.shape[:2]
    slopes = alibi_slopes(ATTN_HEADS)
    outs, lses = [], []
    for g, (window, dilation) in enumerate(ATTN_PATTERNS):
        sl = slice(g * HEADS_PER_PATTERN, (g + 1) * HEADS_PER_PATTERN)
        o, lse = dilated_window_attention(q[:, :, sl], k[:, :, sl], v[:, :, sl], slopes[sl], window, dilation)
        outs.append(o)
        lses.append(lse)
    o = jnp.stack(outs, axis=2)
    lse = jnp.stack(lses, axis=2)
    alpha = jax.nn.softmax(lse, axis=2)
    return (o * alpha[..., None]).reshape(B, S, ATTN_WIDTH).astype(q.dtype)


def conv_module(a, gate, dw_w, dw_b, ln_g, ln_b):
    u = a * jax.nn.sigmoid(gate)
    u = causal_depthwise_conv(u, dw_w, dw_b)
    u = layernorm(u, ln_g, ln_b)
    return jax.nn.silu(u)


def rg_lru(x, w_a, b_a, w_x, b_x, lam):
    B, S, C = x.shape
    xh = x.reshape(B, S, LRU_HEADS, C // LRU_HEADS)
    r = jax.nn.sigmoid(jnp.einsum("bshi,hio->bsho", xh, w_a).reshape(B, S, C) + b_a)
    i = jax.nn.sigmoid(jnp.einsum("bshi,hio->bsho", xh, w_x).reshape(B, S, C) + b_x)
    log_a = -LRU_C * r.astype(jnp.float32) * jax.nn.softplus(-lam.astype(jnp.float32))
    a = jnp.exp(log_a)
    b = jnp.sqrt(-jnp.expm1(2.0 * log_a)) * (i * x).astype(jnp.float32)

    def combine(left, right):
        a1, b1 = left
        a2, b2 = right
        return a1 * a2, a2 * b1 + b2

    _, h = lax.associative_scan(combine, (a, b), axis=1)
    return h.astype(x.dtype)


def recurrent_mixer(gate, xr, cw, cb, w_a, b_a, w_x, b_x, lam):
    u = causal_depthwise_conv(xr, cw, cb)
    return jax.nn.gelu(gate) * rg_lru(u, w_a, b_a, w_x, b_x, lam)


def hybrid_layer(x, norm1_g, w_in, conv_dw_w, conv_dw_b, conv_ln_g, conv_ln_b,
                 lru_conv_w, lru_conv_b, lru_wa, lru_ba, lru_wx, lru_bx, lru_lambda,
                 w_out, norm2_g, w_up, w_down):
    B, S, _ = x.shape
    h = rmsnorm(x, norm1_g)
    z = jnp.einsum("bsd,dc->bsc", h, w_in)
    q = z[..., Q0:K0].reshape(B, S, ATTN_HEADS, HEAD_DIM)
    k = z[..., K0:V0].reshape(B, S, ATTN_HEADS, HEAD_DIM)
    v = z[..., V0:CA0].reshape(B, S, ATTN_HEADS, HEAD_DIM)
    y_attn = attention_mixer(q, k, v)
    y_conv = conv_module(z[..., CA0:CG0], z[..., CG0:LG0], conv_dw_w, conv_dw_b, conv_ln_g, conv_ln_b)
    y_lru = recurrent_mixer(z[..., LG0:LX0], z[..., LX0:IN_COLS], lru_conv_w, lru_conv_b,
                            lru_wa, lru_ba, lru_wx, lru_bx, lru_lambda)
    mix = jnp.concatenate([y_attn, y_conv, y_lru], axis=-1)
    x = x + jnp.einsum("bsc,cd->bsd", mix, w_out)
    h2 = rmsnorm(x, norm2_g)
    ff = jnp.square(jax.nn.relu(jnp.einsum("bsd,df->bsf", h2, w_up)))
    return x + jnp.einsum("bsf,fd->bsd", ff, w_down)


def setup_inputs(seed: int = 0) -> dict:
    key = jax.random.key(seed)
    ks = jax.random.split(key, 20)
    n = jax.random.normal
    f32 = jnp.float32
    x = n(ks[0], (BATCH, SEQ, D_MODEL), f32)
    norm1_g = 1.0 + 0.02 * n(ks[1], (DEPTH, D_MODEL), f32)
    w_in = n(ks[2], (DEPTH, D_MODEL, IN_COLS), f32) * D_MODEL ** -0.5
    conv_dw_w = n(ks[3], (DEPTH, CONV_TAPS, CONV_WIDTH), f32) * CONV_TAPS ** -0.5
    conv_dw_b = 0.02 * n(ks[4], (DEPTH, CONV_WIDTH), f32)
    conv_ln_g = 1.0 + 0.02 * n(ks[5], (DEPTH, CONV_WIDTH), f32)
    conv_ln_b = 0.02 * n(ks[6], (DEPTH, CONV_WIDTH), f32)
    lru_conv_w = n(ks[7], (DEPTH, LRU_CONV_TAPS, LRU_WIDTH), f32) * LRU_CONV_TAPS ** -0.5
    lru_conv_b = 0.02 * n(ks[8], (DEPTH, LRU_WIDTH), f32)
    blk = LRU_WIDTH // LRU_HEADS
    lru_wa = n(ks[9], (DEPTH, LRU_HEADS, blk, blk), f32) * blk ** -0.5
    lru_ba = 0.02 * n(ks[10], (DEPTH, LRU_WIDTH), f32)
    lru_wx = n(ks[11], (DEPTH, LRU_HEADS, blk, blk), f32) * blk ** -0.5
    lru_bx = 0.02 * n(ks[12], (DEPTH, LRU_WIDTH), f32)
    a_c = jax.random.uniform(ks[13], (DEPTH, LRU_WIDTH), f32, 0.9, 0.999)
    a0 = a_c ** (1.0 / LRU_C)
    lru_lambda = jnp.log(a0) - jnp.log1p(-a0)
    w_out = n(ks[14], (DEPTH, MIX_WIDTH, D_MODEL), f32) * MIX_WIDTH ** -0.5
    norm2_g = 1.0 + 0.02 * n(ks[15], (DEPTH, D_MODEL), f32)
    w_up = n(ks[16], (DEPTH, D_MODEL, D_FF), f32) * D_MODEL ** -0.5
    w_down = n(ks[17], (DEPTH, D_FF, D_MODEL), f32) * D_FF ** -0.5
    final_g = 1.0 + 0.02 * n(ks[18], (D_MODEL,), f32)
    return {"x": x, "norm1_g": norm1_g, "w_in": w_in, "conv_dw_w": conv_dw_w, "conv_dw_b": conv_dw_b,
            "conv_ln_g": conv_ln_g, "conv_ln_b": conv_ln_b, "lru_conv_w": lru_conv_w, "lru_conv_b": lru_conv_b,
            "lru_wa": lru_wa, "lru_ba": lru_ba, "lru_wx": lru_wx, "lru_bx": lru_bx, "lru_lambda": lru_lambda,
            "w_out": w_out, "norm2_g": norm2_g, "w_up": w_up, "w_down": w_down, "final_g": final_g}


def reference(x, norm1_g, w_in, conv_dw_w, conv_dw_b, conv_ln_g, conv_ln_b, lru_conv_w, lru_conv_b,
              lru_wa, lru_ba, lru_wx, lru_bx, lru_lambda, w_out, norm2_g, w_up, w_down, final_g):
    for l in range(DEPTH):
        x = hybrid_layer(x, norm1_g[l], w_in[l], conv_dw_w[l], conv_dw_b[l], conv_ln_g[l], conv_ln_b[l],
                         lru_conv_w[l], lru_conv_b[l], lru_wa[l], lru_ba[l], lru_wx[l], lru_bx[l], lru_lambda[l],
                         w_out[l], norm2_g[l], w_up[l], w_down[l])
    return rmsnorm(x, final_g)
```

```python
import functools
import math

import jax
import jax.numpy as jnp
from jax import lax
from jax.experimental import pallas as pl
from jax.experimental.pallas import tpu as pltpu

F32 = jnp.float32
BF16 = jnp.bfloat16

LANES = 128
SUBLANES = 8
VMEM_BYTES = 64 << 20

HEAD_DIM = 64
ATTN_HEADS = 6
ATTN_PATTERNS = ((128, 1), (512, 4), (2048, 16))
ATTN_W = ATTN_HEADS * HEAD_DIM
CONV_W = 4 * HEAD_DIM
CONV_TAPS = 31
LRU_HEADS = 6
LRU_W = LRU_HEADS * HEAD_DIM
LRU_TAPS = 4
LRU_C = 8.0
RMS_EPS = 1e-6
LN_EPS = 1e-5
QKV_W = 3 * ATTN_W
QKV_SLABS = QKV_W // LANES
CONV0 = QKV_W
LRU0 = CONV0 + 2 * CONV_W
BLOCK_W = 128
NEG = -1e30
ALIBI_SLOPES = tuple(2.0 ** (-8.0 * (i + 1) / ATTN_HEADS) for i in range(ATTN_HEADS))


def _sigmoid(x):
    return 0.5 * jnp.tanh(0.5 * x) + 0.5


def _gelu_tanh(x):
    c = math.sqrt(2.0 / math.pi)
    return 0.5 * x * (1.0 + jnp.tanh(c * (x + 0.044715 * (x * x * x))))


def _rms_scale(x):
    return x * lax.rsqrt(jnp.mean(x * x, axis=-1, keepdims=True) + RMS_EPS)


def _const_spec(shape):
    return pl.BlockSpec(shape, lambda i: (0,) * len(shape), pipeline_mode=pl.Buffered(1))


def _params(vmem_mb):
    return pltpu.CompilerParams(dimension_semantics=("arbitrary",), vmem_limit_bytes=vmem_mb << 20)


def _in_proj_kernel(x_ref, g_ref, w_ref, qkv_ref, zc_ref, zl_ref):
    h = (_rms_scale(x_ref[...]) * g_ref[...]).astype(BF16)
    zq = jnp.dot(h, w_ref[:, :QKV_W], preferred_element_type=F32)
    for j in range(QKV_SLABS):
        qkv_ref[0, j] = zq[:, j * LANES:(j + 1) * LANES]
    zc_ref[...] = jnp.dot(h, w_ref[:, CONV0:LRU0], preferred_element_type=F32)
    zl_ref[...] = jnp.dot(h, w_ref[:, LRU0:], preferred_element_type=F32)


def _in_proj(x2d, g, w_bf16, batch, seq, tm=512):
    n, d = x2d.shape
    per_seq = seq // tm
    return pl.pallas_call(
        _in_proj_kernel,
        grid=(n // tm,),
        in_specs=[pl.BlockSpec((tm, d), lambda i: (i, 0)),
                  _const_spec((1, d)),
                  _const_spec(w_bf16.shape)],
        out_specs=[pl.BlockSpec((1, QKV_SLABS, tm, LANES), lambda i: (i // per_seq, 0, i % per_seq, 0)),
                   pl.BlockSpec((tm, 2 * CONV_W), lambda i: (i, 0)),
                   pl.BlockSpec((tm, 2 * LRU_W), lambda i: (i, 0))],
        out_shape=[jax.ShapeDtypeStruct((batch, QKV_SLABS, seq, LANES), F32),
                   jax.ShapeDtypeStruct((n, 2 * CONV_W), F32),
                   jax.ShapeDtypeStruct((n, 2 * LRU_W), F32)],
        compiler_params=_params(40),
        name="in_proj",
    )(x2d, g.reshape(1, d), w_bf16)


def _attn_kernel(qkv_ref, out_ref, o_scr, lse_scr, bias_scr, *, seq):
    w = BLOCK_W
    head0 = lax.broadcasted_iota(jnp.int32, (w, LANES), 1) < HEAD_DIM
    qi = lax.broadcasted_iota(jnp.int32, (w, w), 0)
    kj = lax.broadcasted_iota(jnp.int32, (w, w), 1)
    dist_own = qi - kj
    dist_prev = qi + w - kj

    for g, (window, d) in enumerate(ATTN_PATTERNS):
        assert window // d == w
        nb = seq // d // w
        for j in range(2):
            slope = ALIBI_SLOPES[2 * g + j] * d
            bias_scr[j, :, :w] = jnp.where(dist_prev <= w, -slope * dist_prev.astype(F32), NEG)
            bias_scr[j, :, w:] = jnp.where(dist_own >= 0, -slope * dist_own.astype(F32), NEG)

        def block(base, with_prev, g=g, d=d):
            def rows(start, size):
                return pl.ds(start, size, stride=d) if d > 1 else pl.ds(start, size)
            q = qkv_ref[0, g, rows(base, w), :] * (HEAD_DIM ** -0.5)
            krows = rows(base - d * w, 2 * w) if with_prev else rows(base, w)
            kb = qkv_ref[0, 3 + g, krows, :].astype(BF16)
            vb = qkv_ref[0, 6 + g, krows, :].astype(BF16)
            outs, lses = [], []
            for j in range(2):
                qh = jnp.where(head0 if j == 0 else jnp.logical_not(head0), q, 0.0).astype(BF16)
                s = lax.dot_general(qh, kb, (((1,), (1,)), ((), ())), preferred_element_type=F32)
                s = s + (bias_scr[j] if with_prev else bias_scr[j, :, w:])
                m = jnp.max(s, axis=-1, keepdims=True)
                p = jnp.exp(s - m)
                l = jnp.sum(p, axis=-1, keepdims=True)
                o = jnp.dot(p.astype(BF16), vb, preferred_element_type=F32) * (1.0 / l)
                outs.append(o)
                lses.append(jnp.broadcast_to(m + jnp.log(l), (w, LANES)))
            o_scr[g, rows(base, w), :] = jnp.where(head0, outs[0], outs[1])
            lse_scr[g, rows(base, w), :] = jnp.where(head0, lses[0], lses[1])

        def first_blocks(r, carry, block=block):
            block(r, False)
            return carry

        lax.fori_loop(0, d, first_blocks, 0)
        if nb > 1:
            shift = d.bit_length() - 1

            def later_blocks(idx, carry, block=block, d=d, shift=shift):
                r = idx & (d - 1)
                n = (idx >> shift) + 1
                block(r + n * (d * w), True)
                return carry

            lax.fori_loop(0, d * (nb - 1), later_blocks, 0)

    chunk = 256

    def combine(c, carry):
        rows = pl.ds(pl.multiple_of(c * chunk, chunk), chunk)
        l0, l1, l2 = lse_scr[0, rows, :], lse_scr[1, rows, :], lse_scr[2, rows, :]
        mx = jnp.maximum(jnp.maximum(l0, l1), l2)
        e = [jnp.exp(l0 - mx), jnp.exp(l1 - mx), jnp.exp(l2 - mx)]
        inv = 1.0 / (e[0] + e[1] + e[2])
        for g in range(3):
            out_ref[rows, g * LANES:(g + 1) * LANES] = (o_scr[g, rows, :] * (e[g] * inv)).astype(BF16)
        return carry

    lax.fori_loop(0, seq // chunk, combine, 0)


def _attention(qkv, seq):
    batch = qkv.shape[0]
    return pl.pallas_call(
        functools.partial(_attn_kernel, seq=seq),
        grid=(batch,),
        in_specs=[pl.BlockSpec((1, QKV_SLABS, seq, LANES), lambda b: (b, 0, 0, 0))],
        out_specs=pl.BlockSpec((seq, ATTN_W), lambda b: (b, 0)),
        out_shape=jax.ShapeDtypeStruct((batch * seq, ATTN_W), BF16),
        scratch_shapes=[pltpu.VMEM((3, seq, LANES), F32),
                        pltpu.VMEM((3, seq, LANES), F32),
                        pltpu.VMEM((2, BLOCK_W, 2 * BLOCK_W), F32)],
        compiler_params=_params(48),
        name="attn",
    )(qkv)


CONV_PAD = 32


def _conv_kernel(zc_ref, w_ref, b_ref, lg_ref, lb_ref, out_ref, u_scr, *, seq):
    chunk = 64
    u_scr[0:CONV_PAD, :] = jnp.zeros((CONV_PAD, CONV_W), F32)

    def glu(c, carry):
        r0 = pl.multiple_of(c * 256, 256)
        a = zc_ref[pl.ds(r0, 256), :CONV_W]
        gate = zc_ref[pl.ds(r0, 256), CONV_W:]
        u_scr[pl.ds(CONV_PAD + r0, 256), :] = a * _sigmoid(gate)
        return carry

    lax.fori_loop(0, seq // 256, glu, 0)

    def conv(c, carry):
        r0 = pl.multiple_of(c * chunk, chunk)
        acc = None
        for j in range(SUBLANES):
            part = None
            for a in range(-(-CONV_TAPS // SUBLANES)):
                k = CONV_TAPS - 1 - (SUBLANES * a + j)
                if k < 0:
                    continue
                start = r0 + (CONV_PAD - SUBLANES - SUBLANES * a)
                term = w_ref[k:k + 1, :] * u_scr[pl.ds(start, chunk + SUBLANES), :]
                part = term if part is None else part + term
            if j:
                part = pltpu.roll(part, j, axis=0)
            acc = part if acc is None else acc + part
        acc = acc[SUBLANES:, :] + b_ref[...]
        mu = jnp.mean(acc, axis=-1, keepdims=True)
        xc = acc - mu
        var = jnp.mean(xc * xc, axis=-1, keepdims=True)
        y = xc * lax.rsqrt(var + LN_EPS) * lg_ref[...] + lb_ref[...]
        out_ref[pl.ds(r0, chunk), :] = (y * _sigmoid(y)).astype(BF16)
        return carry

    lax.fori_loop(0, seq // chunk, conv, 0)


def _conv_module(zc, w, b, ln_g, ln_b, seq):
    n = zc.shape[0]
    return pl.pallas_call(
        functools.partial(_conv_kernel, seq=seq),
        grid=(n // seq,),
        in_specs=[pl.BlockSpec((seq, 2 * CONV_W), lambda i: (i, 0)),
                  _const_spec((CONV_TAPS, CONV_W)),
                  _const_spec((1, CONV_W)), _const_spec((1, CONV_W)), _const_spec((1, CONV_W))],
        out_specs=pl.BlockSpec((seq, CONV_W), lambda i: (i, 0)),
        out_shape=jax.ShapeDtypeStruct((n, CONV_W), BF16),
        scratch_shapes=[pltpu.VMEM((CONV_PAD + seq, CONV_W), F32)],
        compiler_params=_params(32),
        name="conv",
    )(zc, w, b.reshape(1, -1), ln_g.reshape(1, -1), ln_b.reshape(1, -1))


LRU_PAD = 8
LRU_SLABS = LRU_W // LANES


def _softplus(x):
    y = jnp.exp(-jnp.abs(x))
    u = 1.0 + y
    log1p_y = jnp.where(u == 1.0, y, jnp.log(u) * (y / jnp.where(u == 1.0, 1.0, u - 1.0)))
    return jnp.maximum(x, 0.0) + log1p_y


def _lru_kernel(zl_ref, cw_ref, cb_ref, wcat_ref, ba_ref, bx_ref, lam_ref, out_ref,
                x_scr, a_scr, b_scr, *, seq):
    chunk = 256
    groups = seq // SUBLANES
    x_scr[0:LRU_PAD, :] = jnp.zeros((LRU_PAD, LRU_W), F32)

    def stage_in(c, carry):
        r0 = pl.multiple_of(c * chunk, chunk)
        x_scr[pl.ds(LRU_PAD + r0, chunk), :] = zl_ref[pl.ds(r0, chunk), LRU_W:]
        return carry

    lax.fori_loop(0, seq // chunk, stage_in, 0)

    neg_c_softplus = -LRU_C * _softplus(-lam_ref[...])

    gchunk = 128

    def gates(c, carry):
        r0 = pl.multiple_of(c * gchunk, gchunk)
        xw = x_scr[pl.ds(r0 + (LRU_PAD - SUBLANES), gchunk + SUBLANES), :]
        u = None
        for s in range(LRU_TAPS):
            part = cw_ref[LRU_TAPS - 1 - s:LRU_TAPS - s, :] * xw
            if s:
                part = pltpu.roll(part, s, axis=0)
            u = part if u is None else u + part
        u = u[SUBLANES:, :] + cb_ref[...]
        ri = jnp.dot(u.astype(BF16), wcat_ref[...], preferred_element_type=F32)
        r = _sigmoid(ri[:, :LRU_W] + ba_ref[...])
        i = _sigmoid(ri[:, LRU_W:] + bx_ref[...])
        log_a = r * neg_c_softplus
        a = jnp.exp(log_a)
        b = jnp.sqrt(-jnp.tanh(log_a) * (a * a + 1.0)) * (i * u)
        for j in range(LRU_SLABS):
            a_scr[j, pl.ds(r0, gchunk), :] = a[:, j * LANES:(j + 1) * LANES]
            b_scr[j, pl.ds(r0, gchunk), :] = b[:, j * LANES:(j + 1) * LANES]
        return carry

    lax.fori_loop(0, seq // gchunk, gates, 0)

    row = lax.broadcasted_iota(jnp.int32, (groups, LANES), 0)
    for s in range(LRU_SLABS):
        def grp(j, s=s):
            return (s, pl.ds(j, groups, stride=SUBLANES), slice(None))
        a_run = a_scr[grp(0)]
        b_run = b_scr[grp(0)]
        for j in range(1, SUBLANES):
            a_j = a_scr[grp(j)]
            b_run = a_j * b_run + b_scr[grp(j)]
            a_run = a_j * a_run
            a_scr[grp(j)] = a_run
            b_scr[grp(j)] = b_run
        sh = 1
        while sh < groups:
            keep = row >= sh
            a_prev = jnp.where(keep, pltpu.roll(a_run, sh, axis=0), 1.0)
            b_prev = jnp.where(keep, pltpu.roll(b_run, sh, axis=0), 0.0)
            b_run = a_run * b_prev + b_run
            a_run = a_run * a_prev
            sh *= 2
        carry_in = jnp.where(row >= 1, pltpu.roll(b_run, 1, axis=0), 0.0)
        for j in range(SUBLANES):
            b_scr[grp(j)] = b_scr[grp(j)] + a_scr[grp(j)] * carry_in

    def emit(c, carry):
        r0 = pl.multiple_of(c * chunk, chunk)
        gate = _gelu_tanh(zl_ref[pl.ds(r0, chunk), :LRU_W])
        for j in range(LRU_SLABS):
            out_ref[pl.ds(r0, chunk), j * LANES:(j + 1) * LANES] = (
                gate[:, j * LANES:(j + 1) * LANES] * b_scr[j, pl.ds(r0, chunk), :]).astype(BF16)
        return carry

    lax.fori_loop(0, seq // chunk, emit, 0)


def _block_diag(w):
    heads, blk, _ = w.shape
    eye = jnp.eye(heads, dtype=w.dtype)
    return jnp.einsum("hio,hg->higo", w, eye).reshape(heads * blk, heads * blk)


def _lru(zl, cw, cb, wa, ba, wx, bx, lam, seq):
    n = zl.shape[0]
    wcat = jnp.concatenate([_block_diag(wa), _block_diag(wx)], axis=1).astype(BF16)
    return pl.pallas_call(
        functools.partial(_lru_kernel, seq=seq),
        grid=(n // seq,),
        in_specs=[pl.BlockSpec((seq, 2 * LRU_W), lambda i: (i, 0)),
                  _const_spec((LRU_TAPS, LRU_W)), _const_spec((1, LRU_W)),
                  _const_spec((LRU_W, 2 * LRU_W)),
                  _const_spec((1, LRU_W)), _const_spec((1, LRU_W)), _const_spec((1, LRU_W))],
        out_specs=pl.BlockSpec((seq, LRU_W), lambda i: (i, 0)),
        out_shape=jax.ShapeDtypeStruct((n, LRU_W), BF16),
        scratch_shapes=[pltpu.VMEM((LRU_PAD + seq, LRU_W), F32),
                        pltpu.VMEM((LRU_SLABS, seq, LANES), F32),
                        pltpu.VMEM((LRU_SLABS, seq, LANES), F32)],
        compiler_params=_params(40),
        name="lru",
    )(zl, cw, cb.reshape(1, -1), wcat, ba.reshape(1, -1), bx.reshape(1, -1), lam.reshape(1, -1))


def _mlp_kernel(ya_ref, yc_ref, yl_ref, x_ref, wo_ref, g2_ref, wup_ref, wdn_ref, fg_ref, out_ref, *, final):
    mix = jnp.concatenate([ya_ref[...], yc_ref[...], yl_ref[...]], axis=-1)
    x1 = x_ref[...] + jnp.dot(mix, wo_ref[...], preferred_element_type=F32)
    h2 = (_rms_scale(x1) * g2_ref[...]).astype(BF16)
    d_ff = wup_ref.shape[1]
    fchunk = 1024
    down = None
    for c in range(d_ff // fchunk):
        up = jnp.dot(h2, wup_ref[:, c * fchunk:(c + 1) * fchunk], preferred_element_type=F32)
        ff = jnp.square(jnp.maximum(up, 0.0)).astype(BF16)
        part = jnp.dot(ff, wdn_ref[c * fchunk:(c + 1) * fchunk, :], preferred_element_type=F32)
        down = part if down is None else down + part
    acc = x1 + down
    if final:
        acc = _rms_scale(acc) * fg_ref[...]
    out_ref[...] = acc


def _mlp(ya, yc, yl, x2d, wo, g2, wup, wdn, fg, final, tm=512):
    n, d = x2d.shape
    row = lambda w: pl.BlockSpec((tm, w), lambda i: (i, 0))
    return pl.pallas_call(
        functools.partial(_mlp_kernel, final=final),
        grid=(n // tm,),
        in_specs=[row(ATTN_W), row(CONV_W), row(LRU_W), row(d),
                  _const_spec(wo.shape), _const_spec((1, d)),
                  _const_spec(wup.shape), _const_spec(wdn.shape), _const_spec((1, d))],
        out_specs=row(d),
        out_shape=jax.ShapeDtypeStruct((n, d), F32),
        compiler_params=_params(56),
        name="mlp",
    )(ya, yc, yl, x2d, wo, g2.reshape(1, d), wup, wdn, fg.reshape(1, d))


def kernel(x, norm1_g, w_in, conv_dw_w, conv_dw_b, conv_ln_g, conv_ln_b, lru_conv_w, lru_conv_b, lru_wa, lru_ba, lru_wx, lru_bx, lru_lambda, w_out, norm2_g, w_up, w_down, final_g):
    batch, seq, d = x.shape
    depth = w_in.shape[0]
    x2d = x.reshape(batch * seq, d)
    for l in range(depth):
        qkv, zc, zl = _in_proj(x2d, norm1_g[l], w_in[l].astype(BF16), batch, seq)
        ya = _attention(qkv, seq)
        yc = _conv_module(zc, conv_dw_w[l], conv_dw_b[l], conv_ln_g[l], conv_ln_b[l], seq)
        yl = _lru(zl, lru_conv_w[l], lru_conv_b[l], lru_wa[l], lru_ba[l], lru_wx[l], lru_bx[l],
                  lru_lambda[l], seq)
        x2d = _mlp(ya, yc, yl, x2d, w_out[l].astype(BF16), norm2_g[l], w_up[l].astype(BF16),
                   w_down[l].astype(BF16), final_g, final=(l == depth - 1))
    return x2d.reshape(batch, seq, d)
```

```python
import functools
import math

import jax
import jax.numpy as jnp
from jax import lax
from jax.experimental import pallas as pl
from jax.experimental.pallas import tpu as pltpu

F32 = jnp.float32
BF16 = jnp.bfloat16

LANES = 128
SUBLANES = 8
VMEM_BYTES = 64 << 20

HEAD_DIM = 64
ATTN_HEADS = 6
ATTN_PATTERNS = ((128, 1), (512, 4), (2048, 16))
ATTN_W = ATTN_HEADS * HEAD_DIM
CONV_W = 4 * HEAD_DIM
CONV_TAPS = 31
LRU_HEADS = 6
LRU_W = LRU_HEADS * HEAD_DIM
LRU_TAPS = 4
LRU_C = 8.0
RMS_EPS = 1e-6
LN_EPS = 1e-5
QKV_W = 3 * ATTN_W
QKV_SLABS = QKV_W // LANES
CONV0 = QKV_W
LRU0 = CONV0 + 2 * CONV_W
BLOCK_W = 128
NEG = -1e30
ALIBI_SLOPES = tuple(2.0 ** (-8.0 * (i + 1) / ATTN_HEADS) for i in range(ATTN_HEADS))


def _sigmoid(x):
    return 0.5 * jnp.tanh(0.5 * x) + 0.5


def _gelu_tanh(x):
    c = math.sqrt(2.0 / math.pi)
    return 0.5 * x * (1.0 + jnp.tanh(c * (x + 0.044715 * (x * x * x))))


def _rms_scale(x):
    return x * lax.rsqrt(jnp.mean(x * x, axis=-1, keepdims=True) + RMS_EPS)


def _const_spec(shape):
    return pl.BlockSpec(shape, lambda i: (0,) * len(shape), pipeline_mode=pl.Buffered(1))


def _params(vmem_mb):
    return pltpu.CompilerParams(dimension_semantics=("arbitrary",), vmem_limit_bytes=vmem_mb << 20)


def _in_proj_kernel(x_ref, g_ref, w_ref, qkv_ref, zc_ref, zl_ref):
    h = (_rms_scale(x_ref[...]) * g_ref[...]).astype(BF16)
    zq = jnp.dot(h, w_ref[:, :QKV_W], preferred_element_type=F32)
    for j in range(QKV_SLABS):
        qkv_ref[0, j] = zq[:, j * LANES:(j + 1) * LANES]
    zc_ref[...] = jnp.dot(h, w_ref[:, CONV0:LRU0], preferred_element_type=F32)
    zl_ref[...] = jnp.dot(h, w_ref[:, LRU0:], preferred_element_type=F32)


def _in_proj(x2d, g, w_bf16, batch, seq, tm=512):
    n, d = x2d.shape
    per_seq = seq // tm
    return pl.pallas_call(
        _in_proj_kernel,
        grid=(n // tm,),
        in_specs=[pl.BlockSpec((tm, d), lambda i: (i, 0)),
                  _const_spec((1, d)),
                  _const_spec(w_bf16.shape)],
        out_specs=[pl.BlockSpec((1, QKV_SLABS, tm, LANES), lambda i: (i // per_seq, 0, i % per_seq, 0)),
                   pl.BlockSpec((tm, 2 * CONV_W), lambda i: (i, 0)),
                   pl.BlockSpec((tm, 2 * LRU_W), lambda i: (i, 0))],
        out_shape=[jax.ShapeDtypeStruct((batch, QKV_SLABS, seq, LANES), F32),
                   jax.ShapeDtypeStruct((n, 2 * CONV_W), F32),
                   jax.ShapeDtypeStruct((n, 2 * LRU_W), F32)],
        compiler_params=_params(40),
        name="in_proj",
    )(x2d, g.reshape(1, d), w_bf16)


def _largest_divisor(n, cap):
    return max(u for u in range(1, cap + 1) if n % u == 0)


def _attn_kernel(qkv_ref, out_ref, o_scr, lse_scr, bias_scr, *, seq):
    w = BLOCK_W
    head0 = lax.broadcasted_iota(jnp.int32, (w, LANES), 1) < HEAD_DIM
    qi = lax.broadcasted_iota(jnp.int32, (w, w), 0)
    kj = lax.broadcasted_iota(jnp.int32, (w, w), 1)
    dist_own = qi - kj
    dist_prev = qi + w - kj
    ones = jnp.ones((2 * w, LANES), BF16)

    for g, (window, d) in enumerate(ATTN_PATTERNS):
        assert window // d == w
        nb = seq // d // w
        for j in range(2):
            slope = ALIBI_SLOPES[2 * g + j] * d
            bias_scr[j * w:(j + 1) * w, :w] = jnp.where(dist_prev <= w, -slope * dist_prev.astype(F32), NEG)
            bias_scr[j * w:(j + 1) * w, w:] = jnp.where(dist_own >= 0, -slope * dist_own.astype(F32), NEG)

        def block(base, with_prev, g=g, d=d):
            def rows(start, size):
                return pl.ds(start, size, stride=d) if d > 1 else pl.ds(start, size)
            nk = 2 * w if with_prev else w
            q = qkv_ref[0, g, rows(base, w), :] * (HEAD_DIM ** -0.5)
            q2 = jnp.concatenate([jnp.where(head0, q, 0.0), jnp.where(head0, 0.0, q)], axis=0).astype(BF16)
            krows = rows(base - d * w, nk) if with_prev else rows(base, nk)
            kb = qkv_ref[0, 3 + g, krows, :].astype(BF16)
            vb = jnp.concatenate([qkv_ref[0, 6 + g, krows, :].astype(BF16), ones[:nk]], axis=1)
            s = lax.dot_general(q2, kb, (((1,), (1,)), ((), ())), preferred_element_type=F32)
            s = s + (bias_scr[...] if with_prev else bias_scr[:, w:])
            m = jnp.max(s, axis=-1, keepdims=True)
            p = jnp.exp(s - m).astype(BF16)
            pv = jnp.dot(p, vb, preferred_element_type=F32)
            o = jnp.where(head0, pv[:w, :LANES], pv[w:, :LANES])
            l = jnp.where(head0, pv[:w, LANES:], pv[w:, LANES:])
            mm = jnp.where(head0, jnp.broadcast_to(m[:w], (w, LANES)), jnp.broadcast_to(m[w:], (w, LANES)))
            o_scr[g, rows(base, w), :] = o * (1.0 / l)
            lse_scr[g, rows(base, w), :] = mm + jnp.log(l)

        u_first = _largest_divisor(d, 8)

        def first_blocks(i, carry, block=block, u_first=u_first):
            for u in range(u_first):
                block(i * u_first + u, False)
            return carry

        lax.fori_loop(0, d // u_first, first_blocks, 0)
        if nb > 1:
            shift = d.bit_length() - 1
            n_later = d * (nb - 1)
            u_later = _largest_divisor(n_later, 6)

            def later_blocks(i, carry, block=block, d=d, shift=shift, u_later=u_later):
                for u in range(u_later):
                    idx = i * u_later + u
                    block((idx & (d - 1)) + ((idx >> shift) + 1) * (d * w), True)
                return carry

            lax.fori_loop(0, n_later // u_later, later_blocks, 0)

    chunk = 256

    def combine(c, carry):
        rows = pl.ds(pl.multiple_of(c * chunk, chunk), chunk)
        l0, l1, l2 = lse_scr[0, rows, :], lse_scr[1, rows, :], lse_scr[2, rows, :]
        mx = jnp.maximum(jnp.maximum(l0, l1), l2)
        e = [jnp.exp(l0 - mx), jnp.exp(l1 - mx), jnp.exp(l2 - mx)]
        inv = 1.0 / (e[0] + e[1] + e[2])
        for g in range(3):
            out_ref[rows, g * LANES:(g + 1) * LANES] = (o_scr[g, rows, :] * (e[g] * inv)).astype(BF16)
        return carry

    lax.fori_loop(0, seq // chunk, combine, 0)


def _attention(qkv, seq):
    batch = qkv.shape[0]
    return pl.pallas_call(
        functools.partial(_attn_kernel, seq=seq),
        grid=(batch,),
        in_specs=[pl.BlockSpec((1, QKV_SLABS, seq, LANES), lambda b: (b, 0, 0, 0))],
        out_specs=pl.BlockSpec((seq, ATTN_W), lambda b: (b, 0)),
        out_shape=jax.ShapeDtypeStruct((batch * seq, ATTN_W), BF16),
        scratch_shapes=[pltpu.VMEM((3, seq, LANES), F32),
                        pltpu.VMEM((3, seq, LANES), F32),
                        pltpu.VMEM((2 * BLOCK_W, 2 * BLOCK_W), F32)],
        compiler_params=_params(48),
        name="attn",
    )(qkv)


CONV_PAD = 32


def _conv_kernel(zc_ref, w_ref, b_ref, lg_ref, lb_ref, out_ref, u_scr, *, seq):
    chunk = 64
    u_scr[0:CONV_PAD, :] = jnp.zeros((CONV_PAD, CONV_W), F32)

    def glu(c, carry):
        r0 = pl.multiple_of(c * 256, 256)
        a = zc_ref[pl.ds(r0, 256), :CONV_W]
        gate = zc_ref[pl.ds(r0, 256), CONV_W:]
        u_scr[pl.ds(CONV_PAD + r0, 256), :] = a * _sigmoid(gate)
        return carry

    lax.fori_loop(0, seq // 256, glu, 0)

    def conv(c, carry):
        r0 = pl.multiple_of(c * chunk, chunk)
        acc = None
        for j in range(SUBLANES):
            part = None
            for a in range(-(-CONV_TAPS // SUBLANES)):
                k = CONV_TAPS - 1 - (SUBLANES * a + j)
                if k < 0:
                    continue
                start = r0 + (CONV_PAD - SUBLANES - SUBLANES * a)
                term = w_ref[k:k + 1, :] * u_scr[pl.ds(start, chunk + SUBLANES), :]
                part = term if part is None else part + term
            if j:
                part = pltpu.roll(part, j, axis=0)
            acc = part if acc is None else acc + part
        acc = acc[SUBLANES:, :] + b_ref[...]
        mu = jnp.mean(acc, axis=-1, keepdims=True)
        xc = acc - mu
        var = jnp.mean(xc * xc, axis=-1, keepdims=True)
        y = xc * lax.rsqrt(var + LN_EPS) * lg_ref[...] + lb_ref[...]
        out_ref[pl.ds(r0, chunk), :] = (y * _sigmoid(y)).astype(BF16)
        return carry

    lax.fori_loop(0, seq // chunk, conv, 0)


def _conv_module(zc, w, b, ln_g, ln_b, seq):
    n = zc.shape[0]
    return pl.pallas_call(
        functools.partial(_conv_kernel, seq=seq),
        grid=(n // seq,),
        in_specs=[pl.BlockSpec((seq, 2 * CONV_W), lambda i: (i, 0)),
                  _const_spec((CONV_TAPS, CONV_W)),
                  _const_spec((1, CONV_W)), _const_spec((1, CONV_W)), _const_spec((1, CONV_W))],
        out_specs=pl.BlockSpec((seq, CONV_W), lambda i: (i, 0)),
        out_shape=jax.ShapeDtypeStruct((n, CONV_W), BF16),
        scratch_shapes=[pltpu.VMEM((CONV_PAD + seq, CONV_W), F32)],
        compiler_params=_params(32),
        name="conv",
    )(zc, w, b.reshape(1, -1), ln_g.reshape(1, -1), ln_b.reshape(1, -1))


LRU_PAD = 8
LRU_SLABS = LRU_W // LANES


def _softplus(x):
    y = jnp.exp(-jnp.abs(x))
    u = 1.0 + y
    log1p_y = jnp.where(u == 1.0, y, jnp.log(u) * (y / jnp.where(u == 1.0, 1.0, u - 1.0)))
    return jnp.maximum(x, 0.0) + log1p_y


def _lru_kernel(zl_ref, cw_ref, cb_ref, wcat_ref, ba_ref, bx_ref, lam_ref, out_ref,
                x_scr, a_scr, b_scr, *, seq):
    chunk = 256
    groups = seq // SUBLANES
    x_scr[0:LRU_PAD, :] = jnp.zeros((LRU_PAD, LRU_W), F32)

    def stage_in(c, carry):
        r0 = pl.multiple_of(c * chunk, chunk)
        x_scr[pl.ds(LRU_PAD + r0, chunk), :] = zl_ref[pl.ds(r0, chunk), LRU_W:]
        return carry

    lax.fori_loop(0, seq // chunk, stage_in, 0)

    neg_c_softplus = -LRU_C * _softplus(-lam_ref[...])

    gchunk = 128

    def gates(c, carry):
        r0 = pl.multiple_of(c * gchunk, gchunk)
        xw = x_scr[pl.ds(r0 + (LRU_PAD - SUBLANES), gchunk + SUBLANES), :]
        u = None
        for s in range(LRU_TAPS):
            part = cw_ref[LRU_TAPS - 1 - s:LRU_TAPS - s, :] * xw
            if s:
                part = pltpu.roll(part, s, axis=0)
            u = part if u is None else u + part
        u = u[SUBLANES:, :] + cb_ref[...]
        ri = jnp.dot(u.astype(BF16), wcat_ref[...], preferred_element_type=F32)
        r = _sigmoid(ri[:, :LRU_W] + ba_ref[...])
        i = _sigmoid(ri[:, LRU_W:] + bx_ref[...])
        log_a = r * neg_c_softplus
        a = jnp.exp(log_a)
        b = jnp.sqrt(-jnp.tanh(log_a) * (a * a + 1.0)) * (i * u)
        for j in range(LRU_SLABS):
            a_scr[j, pl.ds(r0, gchunk), :] = a[:, j * LANES:(j + 1) * LANES]
            b_scr[j, pl.ds(r0, gchunk), :] = b[:, j * LANES:(j + 1) * LANES]
        return carry

    lax.fori_loop(0, seq // gchunk, gates, 0)

    row = lax.broadcasted_iota(jnp.int32, (groups, LANES), 0)
    for s in range(LRU_SLABS):
        def grp(j, s=s):
            return (s, pl.ds(j, groups, stride=SUBLANES), slice(None))
        a_run = a_scr[grp(0)]
        b_run = b_scr[grp(0)]
        for j in range(1, SUBLANES):
            a_j = a_scr[grp(j)]
            b_run = a_j * b_run + b_scr[grp(j)]
            a_run = a_j * a_run
            a_scr[grp(j)] = a_run
            b_scr[grp(j)] = b_run
        sh = 1
        while sh < groups:
            keep = row >= sh
            a_prev = jnp.where(keep, pltpu.roll(a_run, sh, axis=0), 1.0)
            b_prev = jnp.where(keep, pltpu.roll(b_run, sh, axis=0), 0.0)
            b_run = a_run * b_prev + b_run
            a_run = a_run * a_prev
            sh *= 2
        carry_in = jnp.where(row >= 1, pltpu.roll(b_run, 1, axis=0), 0.0)
        for j in range(SUBLANES):
            b_scr[grp(j)] = b_scr[grp(j)] + a_scr[grp(j)] * carry_in

    def emit(c, carry):
        r0 = pl.multiple_of(c * chunk, chunk)
        gate = _gelu_tanh(zl_ref[pl.ds(r0, chunk), :LRU_W])
        for j in range(LRU_SLABS):
            out_ref[pl.ds(r0, chunk), j * LANES:(j + 1) * LANES] = (
                gate[:, j * LANES:(j + 1) * LANES] * b_scr[j, pl.ds(r0, chunk), :]).astype(BF16)
        return carry

    lax.fori_loop(0, seq // chunk, emit, 0)


def _block_diag(w):
    heads, blk, _ = w.shape
    eye = jnp.eye(heads, dtype=w.dtype)
    return jnp.einsum("hio,hg->higo", w, eye).reshape(heads * blk, heads * blk)


def _lru(zl, cw, cb, wa, ba, wx, bx, lam, seq):
    n = zl.shape[0]
    wcat = jnp.concatenate([_block_diag(wa), _block_diag(wx)], axis=1).astype(BF16)
    return pl.pallas_call(
        functools.partial(_lru_kernel, seq=seq),
        grid=(n // seq,),
        in_specs=[pl.BlockSpec((seq, 2 * LRU_W), lambda i: (i, 0)),
                  _const_spec((LRU_TAPS, LRU_W)), _const_spec((1, LRU_W)),
                  _const_spec((LRU_W, 2 * LRU_W)),
                  _const_spec((1, LRU_W)), _const_spec((1, LRU_W)), _const_spec((1, LRU_W))],
        out_specs=pl.BlockSpec((seq, LRU_W), lambda i: (i, 0)),
        out_shape=jax.ShapeDtypeStruct((n, LRU_W), BF16),
        scratch_shapes=[pltpu.VMEM((LRU_PAD + seq, LRU_W), F32),
                        pltpu.VMEM((LRU_SLABS, seq, LANES), F32),
                        pltpu.VMEM((LRU_SLABS, seq, LANES), F32)],
        compiler_params=_params(40),
        name="lru",
    )(zl, cw, cb.reshape(1, -1), wcat, ba.reshape(1, -1), bx.reshape(1, -1), lam.reshape(1, -1))


def _mlp_kernel(ya_ref, yc_ref, yl_ref, x_ref, wo_ref, g2_ref, wup_ref, wdn_ref, fg_ref, out_ref, *, final):
    mix = jnp.concatenate([ya_ref[...], yc_ref[...], yl_ref[...]], axis=-1)
    x1 = x_ref[...] + jnp.dot(mix, wo_ref[...], preferred_element_type=F32)
    h2 = (_rms_scale(x1) * g2_ref[...]).astype(BF16)
    d_ff = wup_ref.shape[1]
    fchunk = 1024
    down = None
    for c in range(d_ff // fchunk):
        up = jnp.dot(h2, wup_ref[:, c * fchunk:(c + 1) * fchunk], preferred_element_type=F32)
        ff = jnp.square(jnp.maximum(up, 0.0)).astype(BF16)
        part = jnp.dot(ff, wdn_ref[c * fchunk:(c + 1) * fchunk, :], preferred_element_type=F32)
        down = part if down is None else down + part
    acc = x1 + down
    if final:
        acc = _rms_scale(acc) * fg_ref[...]
    out_ref[...] = acc


def _mlp(ya, yc, yl, x2d, wo, g2, wup, wdn, fg, final, tm=512):
    n, d = x2d.shape
    row = lambda w: pl.BlockSpec((tm, w), lambda i: (i, 0))
    return pl.pallas_call(
        functools.partial(_mlp_kernel, final=final),
        grid=(n // tm,),
        in_specs=[row(ATTN_W), row(CONV_W), row(LRU_W), row(d),
                  _const_spec(wo.shape), _const_spec((1, d)),
                  _const_spec(wup.shape), _const_spec(wdn.shape), _const_spec((1, d))],
        out_specs=row(d),
        out_shape=jax.ShapeDtypeStruct((n, d), F32),
        compiler_params=_params(56),
        name="mlp",
    )(ya, yc, yl, x2d, wo, g2.reshape(1, d), wup, wdn, fg.reshape(1, d))


def kernel(x, norm1_g, w_in, conv_dw_w, conv_dw_b, conv_ln_g, conv_ln_b, lru_conv_w, lru_conv_b, lru_wa, lru_ba, lru_wx, lru_bx, lru_lambda, w_out, norm2_g, w_up, w_down, final_g):
    batch, seq, d = x.shape
    depth = w_in.shape[0]
    x2d = x.reshape(batch * seq, d)
    for l in range(depth):
        qkv, zc, zl = _in_proj(x2d, norm1_g[l], w_in[l].astype(BF16), batch, seq)
        ya = _attention(qkv, seq)
        yc = _conv_module(zc, conv_dw_w[l], conv_dw_b[l], conv_ln_g[l], conv_ln_b[l], seq)
        yl = _lru(zl, lru_conv_w[l], lru_conv_b[l], lru_wa[l], lru_ba[l], lru_wx[l], lru_bx[l],
                  lru_lambda[l], seq)
        x2d = _mlp(ya, yc, yl, x2d, w_out[l].astype(BF16), norm2_g[l], w_up[l].astype(BF16),
                   w_down[l].astype(BF16), final_g, final=(l == depth - 1))
    return x2d.reshape(batch, seq, d)
```

```python
import functools
import math

import jax
import jax.numpy as jnp
from jax import lax
from jax.experimental import pallas as pl
from jax.experimental.pallas import tpu as pltpu

F32 = jnp.float32
BF16 = jnp.bfloat16

LANES = 128
SUBLANES = 8
VMEM_BYTES = 64 << 20

HEAD_DIM = 64
ATTN_HEADS = 6
ATTN_PATTERNS = ((128, 1), (512, 4), (2048, 16))
ATTN_W = ATTN_HEADS * HEAD_DIM
CONV_W = 4 * HEAD_DIM
CONV_TAPS = 31
LRU_HEADS = 6
LRU_W = LRU_HEADS * HEAD_DIM
LRU_TAPS = 4
LRU_C = 8.0
RMS_EPS = 1e-6
LN_EPS = 1e-5
QKV_W = 3 * ATTN_W
QKV_SLABS = QKV_W // LANES
CONV0 = QKV_W
LRU0 = CONV0 + 2 * CONV_W
BLOCK_W = 128
NEG = -1e30
ALIBI_SLOPES = tuple(2.0 ** (-8.0 * (i + 1) / ATTN_HEADS) for i in range(ATTN_HEADS))


def _sigmoid(x):
    return 0.5 * jnp.tanh(0.5 * x) + 0.5


def _gelu_tanh(x):
    c = math.sqrt(2.0 / math.pi)
    return 0.5 * x * (1.0 + jnp.tanh(c * (x + 0.044715 * (x * x * x))))


def _rms_scale(x):
    return x * lax.rsqrt(jnp.mean(x * x, axis=-1, keepdims=True) + RMS_EPS)


def _const_spec(shape):
    return pl.BlockSpec(shape, lambda i: (0,) * len(shape), pipeline_mode=pl.Buffered(1))


def _params(vmem_mb):
    return pltpu.CompilerParams(dimension_semantics=("arbitrary",), vmem_limit_bytes=vmem_mb << 20)


CONV_PAD = 32
CONV_CHUNK = 64
MXU_DIM = 256
PROJ_PIECE = MXU_DIM


def _conv_chunk(u_scr, r0, w_ref, b_ref, lg_ref, lb_ref):
    acc = None
    for j in range(SUBLANES):
        part = None
        for a in range(-(-CONV_TAPS // SUBLANES)):
            k = CONV_TAPS - 1 - (SUBLANES * a + j)
            if k < 0:
                continue
            start = r0 + (CONV_PAD - SUBLANES - SUBLANES * a)
            term = w_ref[k:k + 1, :] * u_scr[start:start + CONV_CHUNK + SUBLANES, :]
            part = term if part is None else part + term
        if j:
            part = pltpu.roll(part, j, axis=0)
        acc = part if acc is None else acc + part
    acc = acc[SUBLANES:, :] + b_ref[...]
    mu = jnp.mean(acc, axis=-1, keepdims=True)
    xc = acc - mu
    var = jnp.mean(xc * xc, axis=-1, keepdims=True)
    y = xc * lax.rsqrt(var + LN_EPS) * lg_ref[...] + lb_ref[...]
    return (y * _sigmoid(y)).astype(BF16)


def _in_proj_conv_kernel(x_ref, g_ref, w_ref, cw_ref, cb_ref, lg_ref, lb_ref, zero_ref,
                         qkv_ref, zl_ref, yc_ref, zc_scr, u_scr, *, tm, tiles_per_seq):
    i = pl.program_id(0)

    @pl.when(i == 0)
    def _():
        zc_scr[...] = jnp.zeros(zc_scr.shape, F32)
        u_scr[0:CONV_PAD, :] = jnp.zeros((CONV_PAD, CONV_W), F32)

    rows = 128
    for c in range(tm // rows):
        staged = zc_scr[(i + 1) % 2, c * rows:(c + 1) * rows, :]
        u_scr[CONV_PAD + c * rows:CONV_PAD + (c + 1) * rows, :] = staged[:, :CONV_W] * _sigmoid(staged[:, CONV_W:])

    h = (_rms_scale(x_ref[...]) * g_ref[...]).astype(BF16)

    def store_slab(j, val):
        if j < QKV_SLABS:
            qkv_ref[0, j] = val
        elif j < LRU0 // LANES:
            c0 = j * LANES - CONV0
            zc_scr[i % 2, :, c0:c0 + LANES] = val
        else:
            c0 = j * LANES - LRU0
            zl_ref[:, c0:c0 + LANES] = val

    n_cols = w_ref.shape[1]
    pieces = [(lo, min(lo + PROJ_PIECE, n_cols)) for lo in range(0, n_cols, PROJ_PIECE)]
    n_chunks = tm // CONV_CHUNK
    anchor_rows = 2 * SUBLANES
    zero = None
    for step in range(max(len(pieces), n_chunks)):
        if step < len(pieces):
            lo, hi = pieces[step]
            rhs = w_ref[:, lo:hi]
            if zero is not None:
                parts = []
                for k0 in range(0, rhs.shape[0], MXU_DIM):
                    first = (rhs[k0:k0 + anchor_rows, :LANES].astype(F32) + zero).astype(BF16)
                    if hi - lo > LANES:
                        first = jnp.concatenate([first, rhs[k0:k0 + anchor_rows, LANES:]], axis=1)
                    parts.append(first)
                    parts.append(rhs[k0 + anchor_rows:k0 + MXU_DIM, :])
                rhs = jnp.concatenate(parts, axis=0)
            z = jnp.dot(h, rhs, preferred_element_type=F32)
            for j in range(lo // LANES, hi // LANES):
                store_slab(j, z[:, j * LANES - lo:(j + 1) * LANES - lo])
        if step < n_chunks:
            r0 = step * CONV_CHUNK
            y = _conv_chunk(u_scr, r0, cw_ref, cb_ref, lg_ref, lb_ref)
            yc_ref[r0:r0 + CONV_CHUNK, :] = y
            bits = pltpu.bitcast(y[:anchor_rows, :LANES].astype(F32), jnp.int32) & zero_ref[...]
            zero = pltpu.bitcast(bits, F32)
    tail = u_scr[tm:tm + CONV_PAD, :]
    u_scr[0:CONV_PAD, :] = jnp.where(i % tiles_per_seq == 0, 0.0, tail)


def _in_proj_conv(x2d, g, w_bf16, cw, cb, ln_g, ln_b, batch, seq, tm=512):
    n, d = x2d.shape
    per_seq = seq // tm
    last = n // tm - 1
    proj = lambda i: jnp.minimum(i, last)
    return pl.pallas_call(
        functools.partial(_in_proj_conv_kernel, tm=tm, tiles_per_seq=per_seq),
        grid=(n // tm + 1,),
        in_specs=[pl.BlockSpec((tm, d), lambda i: (proj(i), 0)),
                  _const_spec((1, d)),
                  _const_spec(w_bf16.shape),
                  _const_spec((CONV_TAPS, CONV_W)),
                  _const_spec((1, CONV_W)), _const_spec((1, CONV_W)), _const_spec((1, CONV_W)),
                  _const_spec((2 * SUBLANES, LANES))],
        out_specs=[pl.BlockSpec((1, QKV_SLABS, tm, LANES), lambda i: (proj(i) // per_seq, 0, proj(i) % per_seq, 0)),
                   pl.BlockSpec((tm, 2 * LRU_W), lambda i: (proj(i), 0)),
                   pl.BlockSpec((tm, CONV_W), lambda i: (jnp.maximum(i - 1, 0), 0))],
        out_shape=[jax.ShapeDtypeStruct((batch, QKV_SLABS, seq, LANES), F32),
                   jax.ShapeDtypeStruct((n, 2 * LRU_W), F32),
                   jax.ShapeDtypeStruct((n, CONV_W), BF16)],
        scratch_shapes=[pltpu.VMEM((2, tm, 2 * CONV_W), F32),
                        pltpu.VMEM((CONV_PAD + tm, CONV_W), F32)],
        compiler_params=_params(40),
        name="in_proj_conv",
    )(x2d, g.reshape(1, d), w_bf16, cw, cb.reshape(1, -1), ln_g.reshape(1, -1), ln_b.reshape(1, -1),
      jnp.zeros((2 * SUBLANES, LANES), jnp.int32))


def _largest_divisor(n, cap):
    return max(u for u in range(1, cap + 1) if n % u == 0)


def _attn_kernel(qkv_ref, out_ref, o_scr, lse_scr, bias_scr, *, seq):
    w = BLOCK_W
    head0 = lax.broadcasted_iota(jnp.int32, (w, LANES), 1) < HEAD_DIM
    qi = lax.broadcasted_iota(jnp.int32, (w, w), 0)
    kj = lax.broadcasted_iota(jnp.int32, (w, w), 1)
    dist_own = qi - kj
    dist_prev = qi + w - kj
    ones = jnp.ones((2 * w, LANES), BF16)

    for g, (window, d) in enumerate(ATTN_PATTERNS):
        assert window // d == w
        nb = seq // d // w
        for j in range(2):
            slope = ALIBI_SLOPES[2 * g + j] * d
            bias_scr[j * w:(j + 1) * w, :w] = jnp.where(dist_prev <= w, -slope * dist_prev.astype(F32), NEG)
            bias_scr[j * w:(j + 1) * w, w:] = jnp.where(dist_own >= 0, -slope * dist_own.astype(F32), NEG)

        def block(base, with_prev, g=g, d=d):
            def rows(start, size):
                return pl.ds(start, size, stride=d) if d > 1 else pl.ds(start, size)
            nk = 2 * w if with_prev else w
            q = qkv_ref[0, g, rows(base, w), :] * (HEAD_DIM ** -0.5)
            q2 = jnp.concatenate([jnp.where(head0, q, 0.0), jnp.where(head0, 0.0, q)], axis=0).astype(BF16)
            krows = rows(base - d * w, nk) if with_prev else rows(base, nk)
            kb = qkv_ref[0, 3 + g, krows, :].astype(BF16)
            vb = jnp.concatenate([qkv_ref[0, 6 + g, krows, :].astype(BF16), ones[:nk]], axis=1)
            s = lax.dot_general(q2, kb, (((1,), (1,)), ((), ())), preferred_element_type=F32)
            s = s + (bias_scr[...] if with_prev else bias_scr[:, w:])
            m = jnp.max(s, axis=-1, keepdims=True)
            p = jnp.exp(s - m).astype(BF16)
            pv = jnp.dot(p, vb, preferred_element_type=F32)
            o = jnp.where(head0, pv[:w, :LANES], pv[w:, :LANES])
            l = jnp.where(head0, pv[:w, LANES:], pv[w:, LANES:])
            mm = jnp.where(head0, jnp.broadcast_to(m[:w], (w, LANES)), jnp.broadcast_to(m[w:], (w, LANES)))
            o_scr[g, rows(base, w), :] = o * (1.0 / l)
            lse_scr[g, rows(base, w), :] = mm + jnp.log(l)

        u_first = _largest_divisor(d, 8)

        def first_blocks(i, carry, block=block, u_first=u_first):
            for u in range(u_first):
                block(i * u_first + u, False)
            return carry

        lax.fori_loop(0, d // u_first, first_blocks, 0)
        if nb > 1:
            shift = d.bit_length() - 1
            n_later = d * (nb - 1)
            u_later = _largest_divisor(n_later, 6)

            def later_blocks(i, carry, block=block, d=d, shift=shift, u_later=u_later):
                for u in range(u_later):
                    idx = i * u_later + u
                    block((idx & (d - 1)) + ((idx >> shift) + 1) * (d * w), True)
                return carry

            lax.fori_loop(0, n_later // u_later, later_blocks, 0)

    chunk = 256

    def combine(c, carry):
        rows = pl.ds(pl.multiple_of(c * chunk, chunk), chunk)
        l0, l1, l2 = lse_scr[0, rows, :], lse_scr[1, rows, :], lse_scr[2, rows, :]
        mx = jnp.maximum(jnp.maximum(l0, l1), l2)
        e = [jnp.exp(l0 - mx), jnp.exp(l1 - mx), jnp.exp(l2 - mx)]
        inv = 1.0 / (e[0] + e[1] + e[2])
        for g in range(3):
            out_ref[rows, g * LANES:(g + 1) * LANES] = (o_scr[g, rows, :] * (e[g] * inv)).astype(BF16)
        return carry

    lax.fori_loop(0, seq // chunk, combine, 0)


def _attention(qkv, seq):
    batch = qkv.shape[0]
    return pl.pallas_call(
        functools.partial(_attn_kernel, seq=seq),
        grid=(batch,),
        in_specs=[pl.BlockSpec((1, QKV_SLABS, seq, LANES), lambda b: (b, 0, 0, 0))],
        out_specs=pl.BlockSpec((seq, ATTN_W), lambda b: (b, 0)),
        out_shape=jax.ShapeDtypeStruct((batch * seq, ATTN_W), BF16),
        scratch_shapes=[pltpu.VMEM((3, seq, LANES), F32),
                        pltpu.VMEM((3, seq, LANES), F32),
                        pltpu.VMEM((2 * BLOCK_W, 2 * BLOCK_W), F32)],
        compiler_params=_params(48),
        name="attn",
    )(qkv)


LRU_PAD = 8
LRU_SLABS = LRU_W // LANES


def _softplus(x):
    y = jnp.exp(-jnp.abs(x))
    u = 1.0 + y
    log1p_y = jnp.where(u == 1.0, y, jnp.log(u) * (y / jnp.where(u == 1.0, 1.0, u - 1.0)))
    return jnp.maximum(x, 0.0) + log1p_y


def _lru_kernel(zl_ref, cw_ref, cb_ref, wcat_ref, ba_ref, bx_ref, lam_ref, out_ref,
                x_scr, a_scr, b_scr, *, seq):
    chunk = 256
    groups = seq // SUBLANES
    x_scr[0:LRU_PAD, :] = jnp.zeros((LRU_PAD, LRU_W), F32)

    def stage_in(c, carry):
        r0 = pl.multiple_of(c * chunk, chunk)
        x_scr[pl.ds(LRU_PAD + r0, chunk), :] = zl_ref[pl.ds(r0, chunk), LRU_W:]
        return carry

    lax.fori_loop(0, seq // chunk, stage_in, 0)

    neg_c_softplus = -LRU_C * _softplus(-lam_ref[...])

    gchunk = 128

    def gates(c, carry):
        r0 = pl.multiple_of(c * gchunk, gchunk)
        xw = x_scr[pl.ds(r0 + (LRU_PAD - SUBLANES), gchunk + SUBLANES), :]
        u = None
        for s in range(LRU_TAPS):
            part = cw_ref[LRU_TAPS - 1 - s:LRU_TAPS - s, :] * xw
            if s:
                part = pltpu.roll(part, s, axis=0)
            u = part if u is None else u + part
        u = u[SUBLANES:, :] + cb_ref[...]
        ri = jnp.dot(u.astype(BF16), wcat_ref[...], preferred_element_type=F32)
        r = _sigmoid(ri[:, :LRU_W] + ba_ref[...])
        i = _sigmoid(ri[:, LRU_W:] + bx_ref[...])
        log_a = r * neg_c_softplus
        a = jnp.exp(log_a)
        b = jnp.sqrt(-jnp.tanh(log_a) * (a * a + 1.0)) * (i * u)
        for j in range(LRU_SLABS):
            a_scr[j, pl.ds(r0, gchunk), :] = a[:, j * LANES:(j + 1) * LANES]
            b_scr[j, pl.ds(r0, gchunk), :] = b[:, j * LANES:(j + 1) * LANES]
        return carry

    lax.fori_loop(0, seq // gchunk, gates, 0)

    row = lax.broadcasted_iota(jnp.int32, (groups, LANES), 0)
    for s in range(LRU_SLABS):
        def grp(j, s=s):
            return (s, pl.ds(j, groups, stride=SUBLANES), slice(None))
        a_run = a_scr[grp(0)]
        b_run = b_scr[grp(0)]
        for j in range(1, SUBLANES):
            a_j = a_scr[grp(j)]
            b_run = a_j * b_run + b_scr[grp(j)]
            a_run = a_j * a_run
            a_scr[grp(j)] = a_run
            b_scr[grp(j)] = b_run
        sh = 1
        while sh < groups:
            keep = row >= sh
            a_prev = jnp.where(keep, pltpu.roll(a_run, sh, axis=0), 1.0)
            b_prev = jnp.where(keep, pltpu.roll(b_run, sh, axis=0), 0.0)
            b_run = a_run * b_prev + b_run
            a_run = a_run * a_prev
            sh *= 2
        carry_in = jnp.where(row >= 1, pltpu.roll(b_run, 1, axis=0), 0.0)
        for j in range(SUBLANES):
            b_scr[grp(j)] = b_scr[grp(j)] + a_scr[grp(j)] * carry_in

    def emit(c, carry):
        r0 = pl.multiple_of(c * chunk, chunk)
        gate = _gelu_tanh(zl_ref[pl.ds(r0, chunk), :LRU_W])
        for j in range(LRU_SLABS):
            out_ref[pl.ds(r0, chunk), j * LANES:(j + 1) * LANES] = (
                gate[:, j * LANES:(j + 1) * LANES] * b_scr[j, pl.ds(r0, chunk), :]).astype(BF16)
        return carry

    lax.fori_loop(0, seq // chunk, emit, 0)


def _block_diag(w):
    heads, blk, _ = w.shape
    eye = jnp.eye(heads, dtype=w.dtype)
    return jnp.einsum("hio,hg->higo", w, eye).reshape(heads * blk, heads * blk)


def _lru(zl, cw, cb, wa, ba, wx, bx, lam, seq):
    n = zl.shape[0]
    wcat = jnp.concatenate([_block_diag(wa), _block_diag(wx)], axis=1).astype(BF16)
    return pl.pallas_call(
        functools.partial(_lru_kernel, seq=seq),
        grid=(n // seq,),
        in_specs=[pl.BlockSpec((seq, 2 * LRU_W), lambda i: (i, 0)),
                  _const_spec((LRU_TAPS, LRU_W)), _const_spec((1, LRU_W)),
                  _const_spec((LRU_W, 2 * LRU_W)),
                  _const_spec((1, LRU_W)), _const_spec((1, LRU_W)), _const_spec((1, LRU_W))],
        out_specs=pl.BlockSpec((seq, LRU_W), lambda i: (i, 0)),
        out_shape=jax.ShapeDtypeStruct((n, LRU_W), BF16),
        scratch_shapes=[pltpu.VMEM((LRU_PAD + seq, LRU_W), F32),
                        pltpu.VMEM((LRU_SLABS, seq, LANES), F32),
                        pltpu.VMEM((LRU_SLABS, seq, LANES), F32)],
        compiler_params=_params(40),
        name="lru",
    )(zl, cw, cb.reshape(1, -1), wcat, ba.reshape(1, -1), bx.reshape(1, -1), lam.reshape(1, -1))


def _mlp_kernel(ya_ref, yc_ref, yl_ref, x_ref, wo_ref, g2_ref, wup_ref, wdn_ref, fg_ref, out_ref, *, final):
    mix = jnp.concatenate([ya_ref[...], yc_ref[...], yl_ref[...]], axis=-1)
    x1 = x_ref[...] + jnp.dot(mix, wo_ref[...], preferred_element_type=F32)
    h2 = (_rms_scale(x1) * g2_ref[...]).astype(BF16)
    d_ff = wup_ref.shape[1]
    fchunk = 1024
    down = None
    for c in range(d_ff // fchunk):
        up = jnp.dot(h2, wup_ref[:, c * fchunk:(c + 1) * fchunk], preferred_element_type=F32)
        ff = jnp.square(jnp.maximum(up, 0.0)).astype(BF16)
        part = jnp.dot(ff, wdn_ref[c * fchunk:(c + 1) * fchunk, :], preferred_element_type=F32)
        down = part if down is None else down + part
    acc = x1 + down
    if final:
        acc = _rms_scale(acc) * fg_ref[...]
    out_ref[...] = acc


def _mlp(ya, yc, yl, x2d, wo, g2, wup, wdn, fg, final, tm=512):
    n, d = x2d.shape
    row = lambda w: pl.BlockSpec((tm, w), lambda i: (i, 0))
    return pl.pallas_call(
        functools.partial(_mlp_kernel, final=final),
        grid=(n // tm,),
        in_specs=[row(ATTN_W), row(CONV_W), row(LRU_W), row(d),
                  _const_spec(wo.shape), _const_spec((1, d)),
                  _const_spec(wup.shape), _const_spec(wdn.shape), _const_spec((1, d))],
        out_specs=row(d),
        out_shape=jax.ShapeDtypeStruct((n, d), F32),
        compiler_params=_params(56),
        name="mlp",
    )(ya, yc, yl, x2d, wo, g2.reshape(1, d), wup, wdn, fg.reshape(1, d))


def kernel(x, norm1_g, w_in, conv_dw_w, conv_dw_b, conv_ln_g, conv_ln_b, lru_conv_w, lru_conv_b, lru_wa, lru_ba, lru_wx, lru_bx, lru_lambda, w_out, norm2_g, w_up, w_down, final_g):
    batch, seq, d = x.shape
    depth = w_in.shape[0]
    x2d = x.reshape(batch * seq, d)
    for l in range(depth):
        qkv, zl, yc = _in_proj_conv(x2d, norm1_g[l], w_in[l].astype(BF16), conv_dw_w[l], conv_dw_b[l],
                                    conv_ln_g[l], conv_ln_b[l], batch, seq)
        ya = _attention(qkv, seq)
        yl = _lru(zl, lru_conv_w[l], lru_conv_b[l], lru_wa[l], lru_ba[l], lru_wx[l], lru_bx[l],
                  lru_lambda[l], seq)
        x2d = _mlp(ya, yc, yl, x2d, w_out[l].astype(BF16), norm2_g[l], w_up[l].astype(BF16),
                   w_down[l].astype(BF16), final_g, final=(l == depth - 1))
    return x2d.reshape(batch, seq, d)
```

```python
import functools
import math

import jax
import jax.numpy as jnp
from jax import lax
from jax.experimental import pallas as pl
from jax.experimental.pallas import tpu as pltpu

F32 = jnp.float32
BF16 = jnp.bfloat16

LANES = 128
SUBLANES = 8
VMEM_BYTES = 64 << 20

HEAD_DIM = 64
ATTN_HEADS = 6
ATTN_PATTERNS = ((128, 1), (512, 4), (2048, 16))
ATTN_W = ATTN_HEADS * HEAD_DIM
CONV_W = 4 * HEAD_DIM
CONV_TAPS = 31
LRU_HEADS = 6
LRU_W = LRU_HEADS * HEAD_DIM
LRU_TAPS = 4
LRU_C = 8.0
RMS_EPS = 1e-6
LN_EPS = 1e-5
QKV_W = 3 * ATTN_W
QKV_SLABS = QKV_W // LANES
LRU_SLABS = LRU_W // LANES
CONV0 = QKV_W
LRU0 = CONV0 + 2 * CONV_W
BLOCK_W = 128
NEG = -1e30
ALIBI_SLOPES = tuple(2.0 ** (-8.0 * (i + 1) / ATTN_HEADS) for i in range(ATTN_HEADS))


def _sigmoid(x):
    return 0.5 * jnp.tanh(0.5 * x) + 0.5


def _gelu_tanh(x):
    c = math.sqrt(2.0 / math.pi)
    return 0.5 * x * (1.0 + jnp.tanh(c * (x + 0.044715 * (x * x * x))))


def _rms_scale(x):
    return x * lax.rsqrt(jnp.mean(x * x, axis=-1, keepdims=True) + RMS_EPS)


def _const_spec(shape):
    return pl.BlockSpec(shape, lambda i: (0,) * len(shape), pipeline_mode=pl.Buffered(1))


def _params(vmem_mb):
    return pltpu.CompilerParams(dimension_semantics=("arbitrary",), vmem_limit_bytes=vmem_mb << 20)


CONV_PAD = 32
CONV_CHUNK = 64
MXU_DIM = 256
PROJ_PIECE = MXU_DIM


def _conv_chunk(u_scr, r0, w_ref, b_ref, lg_ref, lb_ref):
    halves = []
    for c0 in range(0, CONV_W, LANES):
        acc = None
        for j in range(SUBLANES):
            part = None
            for a in range(-(-CONV_TAPS // SUBLANES)):
                k = CONV_TAPS - 1 - (SUBLANES * a + j)
                if k < 0:
                    continue
                start = r0 + (CONV_PAD - SUBLANES - SUBLANES * a)
                term = w_ref[k:k + 1, c0:c0 + LANES] * u_scr[start:start + CONV_CHUNK + SUBLANES, c0:c0 + LANES]
                part = term if part is None else part + term
            if j:
                part = pltpu.roll(part, j, axis=0)
            acc = part if acc is None else acc + part
        halves.append(acc[SUBLANES:, :])
    acc = jnp.concatenate(halves, axis=1) + b_ref[...]
    mu = jnp.mean(acc, axis=-1, keepdims=True)
    xc = acc - mu
    var = jnp.mean(xc * xc, axis=-1, keepdims=True)
    y = xc * lax.rsqrt(var + LN_EPS) * lg_ref[...] + lb_ref[...]
    return y * _sigmoid(y)


def _in_proj_conv_kernel(x_ref, g_ref, w_ref, cw_ref, cb_ref, lg_ref, lb_ref, zero_ref,
                         qkv_ref, zl_ref, yc_ref, zc_scr, u_scr, *, tm, tiles_per_seq):
    i = pl.program_id(0)

    @pl.when(i == 0)
    def _():
        zc_scr[...] = jnp.zeros(zc_scr.shape, F32)
        u_scr[0:CONV_PAD, :] = jnp.zeros((CONV_PAD, CONV_W), F32)

    rows = 128
    for c in range(tm // rows):
        staged = zc_scr[(i + 1) % 2, c * rows:(c + 1) * rows, :]
        u_scr[CONV_PAD + c * rows:CONV_PAD + (c + 1) * rows, :] = staged[:, :CONV_W] * _sigmoid(staged[:, CONV_W:])

    h = (_rms_scale(x_ref[...]) * g_ref[...]).astype(BF16)

    def store_slab(j, val):
        if j < QKV_SLABS:
            qkv_ref[0, j] = val
        elif j < LRU0 // LANES:
            c0 = j * LANES - CONV0
            zc_scr[i % 2, :, c0:c0 + LANES] = val
        else:
            zl_ref[0, j - LRU0 // LANES] = val

    n_cols = w_ref.shape[1]
    pieces = [(lo, min(lo + PROJ_PIECE, n_cols)) for lo in range(0, n_cols, PROJ_PIECE)]
    n_chunks = tm // CONV_CHUNK
    anchor_rows = 2 * SUBLANES
    zero = None
    for step in range(max(len(pieces), n_chunks)):
        if step < len(pieces):
            lo, hi = pieces[step]
            rhs = w_ref[:, lo:hi]
            if zero is not None:
                parts = []
                for k0 in range(0, rhs.shape[0], MXU_DIM):
                    first = (rhs[k0:k0 + anchor_rows, :LANES].astype(F32) + zero).astype(BF16)
                    if hi - lo > LANES:
                        first = jnp.concatenate([first, rhs[k0:k0 + anchor_rows, LANES:]], axis=1)
                    parts.append(first)
                    parts.append(rhs[k0 + anchor_rows:k0 + MXU_DIM, :])
                rhs = jnp.concatenate(parts, axis=0)
            z = jnp.dot(h, rhs, preferred_element_type=F32)
            for j in range(lo // LANES, hi // LANES):
                store_slab(j, z[:, j * LANES - lo:(j + 1) * LANES - lo])
        if step < n_chunks:
            r0 = step * CONV_CHUNK
            y = _conv_chunk(u_scr, r0, cw_ref, cb_ref, lg_ref, lb_ref)
            yc_ref[r0:r0 + CONV_CHUNK, :] = y.astype(BF16)
            bits = None
            for rr in range(0, CONV_CHUNK, anchor_rows):
                for cc in range(0, CONV_W, LANES):
                    blk = pltpu.bitcast(y[rr:rr + anchor_rows, cc:cc + LANES], jnp.int32)
                    bits = blk if bits is None else bits | blk
            zero = pltpu.bitcast(bits & zero_ref[...], F32)
    tail = u_scr[tm:tm + CONV_PAD, :]
    u_scr[0:CONV_PAD, :] = jnp.where(i % tiles_per_seq == 0, 0.0, tail)


def _in_proj_conv(x2d, g, w_bf16, cw, cb, ln_g, ln_b, batch, seq, tm=512):
    n, d = x2d.shape
    per_seq = seq // tm
    last = n // tm - 1
    proj = lambda i: jnp.minimum(i, last)
    return pl.pallas_call(
        functools.partial(_in_proj_conv_kernel, tm=tm, tiles_per_seq=per_seq),
        grid=(n // tm + 1,),
        in_specs=[pl.BlockSpec((tm, d), lambda i: (proj(i), 0)),
                  _const_spec((1, d)),
                  _const_spec(w_bf16.shape),
                  _const_spec((CONV_TAPS, CONV_W)),
                  _const_spec((1, CONV_W)), _const_spec((1, CONV_W)), _const_spec((1, CONV_W)),
                  _const_spec((2 * SUBLANES, LANES))],
        out_specs=[pl.BlockSpec((1, QKV_SLABS, tm, LANES), lambda i: (proj(i) // per_seq, 0, proj(i) % per_seq, 0)),
                   pl.BlockSpec((1, 2 * LRU_SLABS, tm, LANES),
                                lambda i: (proj(i) // per_seq, 0, proj(i) % per_seq, 0)),
                   pl.BlockSpec((tm, CONV_W), lambda i: (jnp.maximum(i - 1, 0), 0))],
        out_shape=[jax.ShapeDtypeStruct((batch, QKV_SLABS, seq, LANES), F32),
                   jax.ShapeDtypeStruct((batch, 2 * LRU_SLABS, seq, LANES), F32),
                   jax.ShapeDtypeStruct((n, CONV_W), BF16)],
        scratch_shapes=[pltpu.VMEM((2, tm, 2 * CONV_W), F32),
                        pltpu.VMEM((CONV_PAD + tm, CONV_W), F32)],
        compiler_params=_params(40),
        name="in_proj_conv",
    )(x2d, g.reshape(1, d), w_bf16, cw, cb.reshape(1, -1), ln_g.reshape(1, -1), ln_b.reshape(1, -1),
      jnp.zeros((2 * SUBLANES, LANES), jnp.int32))


def _largest_divisor(n, cap):
    return max(u for u in range(1, cap + 1) if n % u == 0)


def _attn_kernel(qkv_ref, out_ref, o_scr, lse_scr, bias_scr, *, seq):
    w = BLOCK_W
    head0 = lax.broadcasted_iota(jnp.int32, (w, LANES), 1) < HEAD_DIM
    qi = lax.broadcasted_iota(jnp.int32, (w, w), 0)
    kj = lax.broadcasted_iota(jnp.int32, (w, w), 1)
    dist_own = qi - kj
    dist_prev = qi + w - kj
    ones = jnp.ones((2 * w, LANES), BF16)

    for g, (window, d) in enumerate(ATTN_PATTERNS):
        assert window // d == w
        nb = seq // d // w
        for j in range(2):
            slope = ALIBI_SLOPES[2 * g + j] * d
            bias_scr[j * w:(j + 1) * w, :w] = jnp.where(dist_prev <= w, -slope * dist_prev.astype(F32), NEG)
            bias_scr[j * w:(j + 1) * w, w:] = jnp.where(dist_own >= 0, -slope * dist_own.astype(F32), NEG)

        def block(base, with_prev, g=g, d=d):
            def rows(start, size):
                return pl.ds(start, size, stride=d) if d > 1 else pl.ds(start, size)
            nk = 2 * w if with_prev else w
            q = qkv_ref[0, g, rows(base, w), :] * (HEAD_DIM ** -0.5)
            q2 = jnp.concatenate([jnp.where(head0, q, 0.0), jnp.where(head0, 0.0, q)], axis=0).astype(BF16)
            krows = rows(base - d * w, nk) if with_prev else rows(base, nk)
            kb = qkv_ref[0, 3 + g, krows, :].astype(BF16)
            vb = jnp.concatenate([qkv_ref[0, 6 + g, krows, :].astype(BF16), ones[:nk]], axis=1)
            s = lax.dot_general(q2, kb, (((1,), (1,)), ((), ())), preferred_element_type=F32)
            s = s + (bias_scr[...] if with_prev else bias_scr[:, w:])
            m = jnp.max(s, axis=-1, keepdims=True)
            p = jnp.exp(s - m).astype(BF16)
            pv = jnp.dot(p, vb, preferred_element_type=F32)
            o = jnp.where(head0, pv[:w, :LANES], pv[w:, :LANES])
            l = jnp.where(head0, pv[:w, LANES:], pv[w:, LANES:])
            mm = jnp.where(head0, jnp.broadcast_to(m[:w], (w, LANES)), jnp.broadcast_to(m[w:], (w, LANES)))
            o_scr[g, rows(base, w), :] = o * (1.0 / l)
            lse_scr[g, rows(base, w), :] = mm + jnp.log(l)

        u_first = _largest_divisor(d, 8)

        def first_blocks(i, carry, block=block, u_first=u_first):
            for u in range(u_first):
                block(i * u_first + u, False)
            return carry

        lax.fori_loop(0, d // u_first, first_blocks, 0)
        if nb > 1:
            shift = d.bit_length() - 1
            n_later = d * (nb - 1)
            u_later = _largest_divisor(n_later, 6)

            def later_blocks(i, carry, block=block, d=d, shift=shift, u_later=u_later):
                for u in range(u_later):
                    idx = i * u_later + u
                    block((idx & (d - 1)) + ((idx >> shift) + 1) * (d * w), True)
                return carry

            lax.fori_loop(0, n_later // u_later, later_blocks, 0)

    chunk = 256

    def combine(c, carry):
        rows = pl.ds(pl.multiple_of(c * chunk, chunk), chunk)
        l0, l1, l2 = lse_scr[0, rows, :], lse_scr[1, rows, :], lse_scr[2, rows, :]
        mx = jnp.maximum(jnp.maximum(l0, l1), l2)
        e = [jnp.exp(l0 - mx), jnp.exp(l1 - mx), jnp.exp(l2 - mx)]
        inv = 1.0 / (e[0] + e[1] + e[2])
        for g in range(3):
            out_ref[rows, g * LANES:(g + 1) * LANES] = (o_scr[g, rows, :] * (e[g] * inv)).astype(BF16)
        return carry

    lax.fori_loop(0, seq // chunk, combine, 0)


def _attention(qkv, seq):
    batch = qkv.shape[0]
    return pl.pallas_call(
        functools.partial(_attn_kernel, seq=seq),
        grid=(batch,),
        in_specs=[pl.BlockSpec((1, QKV_SLABS, seq, LANES), lambda b: (b, 0, 0, 0))],
        out_specs=pl.BlockSpec((seq, ATTN_W), lambda b: (b, 0)),
        out_shape=jax.ShapeDtypeStruct((batch * seq, ATTN_W), BF16),
        scratch_shapes=[pltpu.VMEM((3, seq, LANES), F32),
                        pltpu.VMEM((3, seq, LANES), F32),
                        pltpu.VMEM((2 * BLOCK_W, 2 * BLOCK_W), F32)],
        compiler_params=_params(48),
        name="attn",
    )(qkv)


def _softplus(x):
    y = jnp.exp(-jnp.abs(x))
    u = 1.0 + y
    log1p_y = jnp.where(u == 1.0, y, jnp.log(u) * (y / jnp.where(u == 1.0, 1.0, u - 1.0)))
    return jnp.maximum(x, 0.0) + log1p_y


def _lru_kernel(zl_ref, cw_ref, cb_ref, wcat_ref, ba_ref, bx_ref, lam_ref, out_ref,
                xp_scr, a_scr, b_scr, agg_scr, h_scr, *, seq):
    groups = seq // SUBLANES
    n_back = LRU_TAPS - 1
    row = lax.broadcasted_iota(jnp.int32, (groups, LANES), 0)

    for j in range(SUBLANES):
        for s in range(LRU_SLABS):
            xj = zl_ref[0, LRU_SLABS + s, pl.ds(j, groups, stride=SUBLANES), :]
            xp_scr[j, :, s * LANES:(s + 1) * LANES] = xj
            if j >= SUBLANES - n_back:
                xp_scr[j + n_back, :, s * LANES:(s + 1) * LANES] = jnp.where(
                    row >= 1, pltpu.roll(xj, 1, axis=0), 0.0)

    neg_c_softplus = -LRU_C * _softplus(-lam_ref[...])
    gchunk = 128
    per_block = groups // gchunk

    def gates(idx, carry):
        j = idx // per_block
        r0 = pl.multiple_of((idx % per_block) * gchunk, gchunk)
        u = jnp.broadcast_to(cb_ref[...], (gchunk, LRU_W))
        for s in range(LRU_TAPS):
            src = j if s == 0 else jnp.where(j >= s, j - s, j - s + SUBLANES + n_back)
            u = u + cw_ref[LRU_TAPS - 1 - s:LRU_TAPS - s, :] * xp_scr[src, pl.ds(r0, gchunk), :]
        ri = jnp.dot(u.astype(BF16), wcat_ref[...], preferred_element_type=F32)
        r = _sigmoid(ri[:, :LRU_W] + ba_ref[...])
        i = _sigmoid(ri[:, LRU_W:] + bx_ref[...])
        log_a = r * neg_c_softplus
        a = jnp.exp(log_a)
        z = -jnp.tanh(log_a) * (a * a + 1.0)
        b = jnp.where(z > 0.0, z * lax.rsqrt(z), 0.0) * (i * u)
        for s in range(LRU_SLABS):
            a_scr[j, s, pl.ds(r0, gchunk), :] = a[:, s * LANES:(s + 1) * LANES]
            b_scr[j, s, pl.ds(r0, gchunk), :] = b[:, s * LANES:(s + 1) * LANES]
        return carry

    lax.fori_loop(0, SUBLANES * per_block, gates, 0)

    schunk = 64
    per_slab = groups // schunk

    def fold(idx, carry):
        s = idx // per_slab
        rows = pl.ds(pl.multiple_of((idx % per_slab) * schunk, schunk), schunk)
        a_run = a_scr[0, s, rows, :]
        b_run = b_scr[0, s, rows, :]
        for j in range(1, SUBLANES):
            a_j = a_scr[j, s, rows, :]
            b_run = a_j * b_run + b_scr[j, s, rows, :]
            a_run = a_j * a_run
        agg_scr[0, s, rows, :] = a_run
        agg_scr[1, s, rows, :] = b_run
        return carry

    lax.fori_loop(0, LRU_SLABS * per_slab, fold, 0)

    for s in range(LRU_SLABS):
        a_run = agg_scr[0, s]
        b_run = agg_scr[1, s]
        sh = 1
        while sh < groups:
            keep = row >= sh
            a_prev = jnp.where(keep, pltpu.roll(a_run, sh, axis=0), 1.0)
            b_prev = jnp.where(keep, pltpu.roll(b_run, sh, axis=0), 0.0)
            b_run = a_run * b_prev + b_run
            a_run = a_run * a_prev
            sh *= 2
        agg_scr[1, s] = jnp.where(row >= 1, pltpu.roll(b_run, 1, axis=0), 0.0)

    def replay(idx, carry):
        s = idx // per_slab
        c0 = pl.multiple_of((idx % per_slab) * schunk, schunk)
        rows = pl.ds(c0, schunk)
        h = agg_scr[1, s, rows, :]
        for j in range(SUBLANES):
            h = a_scr[j, s, rows, :] * h + b_scr[j, s, rows, :]
            h_scr[s, pl.ds(c0 * SUBLANES + j, schunk, stride=SUBLANES), :] = h
        return carry

    lax.fori_loop(0, LRU_SLABS * per_slab, replay, 0)

    chunk = 256

    def emit(c, carry):
        rows = pl.ds(pl.multiple_of(c * chunk, chunk), chunk)
        for s in range(LRU_SLABS):
            out_ref[rows, s * LANES:(s + 1) * LANES] = (
                _gelu_tanh(zl_ref[0, s, rows, :]) * h_scr[s, rows, :]).astype(BF16)
        return carry

    lax.fori_loop(0, seq // chunk, emit, 0)


def _block_diag(w):
    heads, blk, _ = w.shape
    eye = jnp.eye(heads, dtype=w.dtype)
    return jnp.einsum("hio,hg->higo", w, eye).reshape(heads * blk, heads * blk)


def _lru(zl, cw, cb, wa, ba, wx, bx, lam, seq):
    batch = zl.shape[0]
    groups = seq // SUBLANES
    wcat = jnp.concatenate([_block_diag(wa), _block_diag(wx)], axis=1).astype(BF16)
    return pl.pallas_call(
        functools.partial(_lru_kernel, seq=seq),
        grid=(batch,),
        in_specs=[pl.BlockSpec((1, 2 * LRU_SLABS, seq, LANES), lambda i: (i, 0, 0, 0)),
                  _const_spec((LRU_TAPS, LRU_W)), _const_spec((1, LRU_W)),
                  _const_spec((LRU_W, 2 * LRU_W)),
                  _const_spec((1, LRU_W)), _const_spec((1, LRU_W)), _const_spec((1, LRU_W))],
        out_specs=pl.BlockSpec((seq, LRU_W), lambda i: (i, 0)),
        out_shape=jax.ShapeDtypeStruct((batch * seq, LRU_W), BF16),
        scratch_shapes=[pltpu.VMEM((SUBLANES + LRU_TAPS - 1, groups, LRU_W), F32),
                        pltpu.VMEM((SUBLANES, LRU_SLABS, groups, LANES), F32),
                        pltpu.VMEM((SUBLANES, LRU_SLABS, groups, LANES), F32),
                        pltpu.VMEM((2, LRU_SLABS, groups, LANES), F32),
                        pltpu.VMEM((LRU_SLABS, seq, LANES), F32)],
        compiler_params=_params(40),
        name="lru",
    )(zl, cw, cb.reshape(1, -1), wcat, ba.reshape(1, -1), bx.reshape(1, -1), lam.reshape(1, -1))


def _mlp_kernel(ya_ref, yc_ref, yl_ref, x_ref, wo_ref, g2_ref, wup_ref, wdn_ref, fg_ref, out_ref, *, final):
    mix = jnp.concatenate([ya_ref[...], yc_ref[...], yl_ref[...]], axis=-1)
    x1 = x_ref[...] + jnp.dot(mix, wo_ref[...], preferred_element_type=F32)
    h2 = (_rms_scale(x1) * g2_ref[...]).astype(BF16)
    d_ff = wup_ref.shape[1]
    fchunk = 1024
    down = None
    for c in range(d_ff // fchunk):
        up = jnp.dot(h2, wup_ref[:, c * fchunk:(c + 1) * fchunk], preferred_element_type=F32)
        ff = jnp.square(jnp.maximum(up, 0.0)).astype(BF16)
        part = jnp.dot(ff, wdn_ref[c * fchunk:(c + 1) * fchunk, :], preferred_element_type=F32)
        down = part if down is None else down + part
    acc = x1 + down
    if final:
        acc = _rms_scale(acc) * fg_ref[...]
    out_ref[...] = acc


def _mlp(ya, yc, yl, x2d, wo, g2, wup, wdn, fg, final, tm=512):
    n, d = x2d.shape
    row = lambda w: pl.BlockSpec((tm, w), lambda i: (i, 0))
    return pl.pallas_call(
        functools.partial(_mlp_kernel, final=final),
        grid=(n // tm,),
        in_specs=[row(ATTN_W), row(CONV_W), row(LRU_W), row(d),
                  _const_spec(wo.shape), _const_spec((1, d)),
                  _const_spec(wup.shape), _const_spec(wdn.shape), _const_spec((1, d))],
        out_specs=row(d),
        out_shape=jax.ShapeDtypeStruct((n, d), F32),
        compiler_params=_params(56),
        name="mlp",
    )(ya, yc, yl, x2d, wo, g2.reshape(1, d), wup, wdn, fg.reshape(1, d))


def kernel(x, norm1_g, w_in, conv_dw_w, conv_dw_b, conv_ln_g, conv_ln_b, lru_conv_w, lru_conv_b, lru_wa, lru_ba, lru_wx, lru_bx, lru_lambda, w_out, norm2_g, w_up, w_down, final_g):
    batch, seq, d = x.shape
    depth = w_in.shape[0]
    x2d = x.reshape(batch * seq, d)
    for l in range(depth):
        qkv, zl, yc = _in_proj_conv(x2d, norm1_g[l], w_in[l].astype(BF16), conv_dw_w[l], conv_dw_b[l],
                                    conv_ln_g[l], conv_ln_b[l], batch, seq)
        ya = _attention(qkv, seq)
        yl = _lru(zl, lru_conv_w[l], lru_conv_b[l], lru_wa[l], lru_ba[l], lru_wx[l], lru_bx[l],
                  lru_lambda[l], seq)
        x2d = _mlp(ya, yc, yl, x2d, w_out[l].astype(BF16), norm2_g[l], w_up[l].astype(BF16),
                   w_down[l].astype(BF16), final_g, final=(l == depth - 1))
    return x2d.reshape(batch, seq, d)
```
